```python
import math
import jax, jax.numpy as jnp
from jax import lax
import numpy as np

D_MODEL = 1024
BATCH = 8
SEQ = 2048
DEPTH = 4

N_A_LAYERS = DEPTH // 2
N_B_LAYERS = DEPTH - N_A_LAYERS
CONV_WIDTH = 31
D_FF = 2816
FFN_CONV_WIDTH = 3
N_HEADS = 8
HEAD_DIM = 64
V_DIM = 2 * HEAD_DIM
Q_BLOCK = 128
EPS = 1e-6
NEG_INF = -1e30

kernel_name = "yoco_conformer_diffattn_convffn"


def lambda_init_fn(layer_idx):
    return 0.8 - 0.6 * math.exp(-0.3 * layer_idx)


def rms_norm(x, g):
    xf = x.astype(jnp.float32)
    y = xf * lax.rsqrt(jnp.mean(xf * xf, axis=-1, keepdims=True) + EPS)
    return (y * g.astype(jnp.float32)).astype(x.dtype)


def layer_norm(x, g, b):
    xf = x.astype(jnp.float32)
    mu = jnp.mean(xf, axis=-1, keepdims=True)
    xc = xf - mu
    y = xc * lax.rsqrt(jnp.mean(xc * xc, axis=-1, keepdims=True) + EPS)
    return (y * g.astype(jnp.float32) + b.astype(jnp.float32)).astype(x.dtype)


def causal_depthwise_conv(u, w, b):
    width, c = w.shape
    y = lax.conv_general_dilated(
        u, w.reshape(width, 1, c).astype(u.dtype), window_strides=(1,),
        padding=[(width - 1, 0)], dimension_numbers=('NWC', 'WIO', 'NWC'),
        feature_group_count=c)
    return y + b.astype(u.dtype)


def conformer_conv_module(x, norm_g, pw1_w, pw1_b, dw_w, dw_b, ln_g, ln_b, pw2_w, pw2_b):
    h = rms_norm(x, norm_g)
    u = h @ pw1_w + pw1_b
    u = u[..., :D_MODEL] * jax.nn.sigmoid(u[..., D_MODEL:])
    u = causal_depthwise_conv(u, dw_w, dw_b)
    u = jax.nn.silu(layer_norm(u, ln_g, ln_b))
    return x + u @ pw2_w + pw2_b


def conv_ffn(x, norm_g, w_in, dw_w, dw_b, w_out):
    h = rms_norm(x, norm_g)
    u = causal_depthwise_conv(h @ w_in, dw_w, dw_b)
    gate, up = u[..., :D_FF], u[..., D_FF:]
    return x + (jax.nn.silu(gate) * up) @ w_out


def diff_attention(q, k, v, lam):
    b, s = q.shape[0], q.shape[1]
    nb = s // Q_BLOCK
    scale = HEAD_DIM ** -0.5
    q_blocks = jnp.moveaxis(q.reshape(b, nb, Q_BLOCK, N_HEADS, 2, HEAD_DIM), 1, 0)
    key_pos = jnp.arange(s)
    vf = v.astype(jnp.float32)

    def one_block(args):
        qb, start = args
        scores = jnp.einsum('bqhcd,bkhcd->bhcqk', qb, k,
                            preferred_element_type=jnp.float32) * scale
        q_pos = start + jnp.arange(Q_BLOCK)
        causal = key_pos[None, :] <= q_pos[:, None]
        probs = jax.nn.softmax(jnp.where(causal, scores, NEG_INF), axis=-1)
        diff = probs[:, :, 0] - lam * probs[:, :, 1]
        return jnp.einsum('bhqk,bkhe->bqhe', diff, vf)

    out = lax.map(one_block, (q_blocks, jnp.arange(nb) * Q_BLOCK))
    return jnp.moveaxis(out, 0, 1).reshape(b, s, N_HEADS, V_DIM)


def setup_inputs(seed: int = 0) -> dict:
    key = jax.random.key(seed)
    ks = iter(jax.random.split(key, 40))

    def nrm(shape, scale):
        return jax.random.normal(next(ks), shape, jnp.float32) * scale

    def gain(shape):
        return 1.0 + 0.05 * jax.random.normal(next(ks), shape, jnp.float32)

    na, nbl, d = N_A_LAYERS, N_B_LAYERS, D_MODEL
    qk_width = N_HEADS * 2 * HEAD_DIM
    v_width = N_HEADS * V_DIM
    return {
        "x": nrm((BATCH, SEQ, d), 1.0),
        "conv_norm_g": gain((na, d)),
        "conv_pw1_w": nrm((na, d, 2 * d), d ** -0.5),
        "conv_pw1_b": nrm((na, 2 * d), 0.02),
        "conv_dw_w": nrm((na, CONV_WIDTH, d), CONV_WIDTH ** -0.5),
        "conv_dw_b": nrm((na, d), 0.02),
        "conv_ln_g": gain((na, d)),
        "conv_ln_b": nrm((na, d), 0.02),
        "conv_pw2_w": nrm((na, d, d), d ** -0.5),
        "conv_pw2_b": nrm((na, d), 0.02),
        "kv_norm_g": gain((d,)),
        "w_k": nrm((d, qk_width), d ** -0.5),
        "w_v": nrm((d, v_width), d ** -0.5),
        "k_norm_g": gain((HEAD_DIM,)),
        "attn_norm_g": gain((nbl, d)),
        "w_q": nrm((nbl, d, qk_width), d ** -0.5),
        "q_norm_g": gain((nbl, HEAD_DIM)),
        "lambda_q1": nrm((nbl, HEAD_DIM), 0.1),
        "lambda_k1": nrm((nbl, HEAD_DIM), 0.1),
        "lambda_q2": nrm((nbl, HEAD_DIM), 0.1),
        "lambda_k2": nrm((nbl, HEAD_DIM), 0.1),
        "subln_g": gain((nbl, V_DIM)),
        "w_o": nrm((nbl, v_width, d), v_width ** -0.5),
        "ffn_norm_g": gain((DEPTH, d)),
        "ffn_w_in": nrm((DEPTH, d, 2 * D_FF), d ** -0.5),
        "ffn_dw_w": nrm((DEPTH, FFN_CONV_WIDTH, 2 * D_FF), FFN_CONV_WIDTH ** -0.5),
        "ffn_dw_b": nrm((DEPTH, 2 * D_FF), 0.02),
        "ffn_w_out": nrm((DEPTH, D_FF, d), D_FF ** -0.5),
    }


def reference(x, conv_norm_g, conv_pw1_w, conv_pw1_b, conv_dw_w, conv_dw_b, conv_ln_g, conv_ln_b,
              conv_pw2_w, conv_pw2_b, kv_norm_g, w_k, w_v, k_norm_g, attn_norm_g, w_q, q_norm_g,
              lambda_q1, lambda_k1, lambda_q2, lambda_k2, subln_g, w_o, ffn_norm_g, ffn_w_in,
              ffn_dw_w, ffn_dw_b, ffn_w_out):
    b, s, _ = x.shape
    for layer in range(DEPTH):
        if layer < N_A_LAYERS:
            i = layer
            x = conformer_conv_module(x, conv_norm_g[i], conv_pw1_w[i], conv_pw1_b[i], conv_dw_w[i],
                                      conv_dw_b[i], conv_ln_g[i], conv_ln_b[i], conv_pw2_w[i],
                                      conv_pw2_b[i])
        else:
            j = layer - N_A_LAYERS
            if j == 0:
                hk = rms_norm(x, kv_norm_g)
                k_sh = rms_norm((hk @ w_k).reshape(b, s, N_HEADS, 2, HEAD_DIM), k_norm_g)
                v_sh = (hk @ w_v).reshape(b, s, N_HEADS, V_DIM)
            lam_init = lambda_init_fn(layer)
            lam = (jnp.exp(jnp.sum(lambda_q1[j].astype(jnp.float32) * lambda_k1[j].astype(jnp.float32)))
                   - jnp.exp(jnp.sum(lambda_q2[j].astype(jnp.float32) * lambda_k2[j].astype(jnp.float32)))
                   + lam_init)
            h = rms_norm(x, attn_norm_g[j])
            q = rms_norm((h @ w_q[j]).reshape(b, s, N_HEADS, 2, HEAD_DIM), q_norm_g[j])
            o = diff_attention(q, k_sh, v_sh, lam)
            o = rms_norm(o, subln_g[j]) * (1.0 - lam_init)
            x = x + (o.reshape(b, s, N_HEADS * V_DIM) @ w_o[j].astype(jnp.float32)).astype(x.dtype)
        x = conv_ffn(x, ffn_norm_g[layer], ffn_w_in[layer], ffn_dw_w[layer], ffn_dw_b[layer],
                     ffn_w_out[layer])
    return x
```

```python
import functools
import math

import jax
import jax.numpy as jnp
from jax import lax
from jax.experimental import pallas as pl
from jax.experimental.pallas import tpu as pltpu

D_MODEL = 1024
DEPTH = 4
N_A_LAYERS = DEPTH // 2
CONV_WIDTH = 31
D_FF = 2816
FFN_CONV_WIDTH = 3
N_HEADS = 8
HEAD_DIM = 64
V_DIM = 2 * HEAD_DIM
EPS = 1e-6
NEG_INF = -1e30

V7X_LANES = 128
V7X_SUBLANES = 8
V7X_VMEM_LIMIT_BYTES = 56 * 1024 * 1024

CONV_HALO = 32
CONV_ROWS_PER_STEP = 32
TQ_CONV = 512
TQ_FFN = 512
TQ_PROJ = 512
FFN_CHUNK = 256
Q_BLOCK_ATTN = 256

BF16 = jnp.bfloat16
F32 = jnp.float32


def _lambda_init(layer_idx):
    return 0.8 - 0.6 * math.exp(-0.3 * layer_idx)


def _rms(x, g):
    ms = jnp.mean(x * x, axis=-1, keepdims=True)
    return x * lax.rsqrt(ms + EPS) * g


def _sigmoid(x):
    return 1.0 / (1.0 + jnp.exp(-x))


def _dot(a, b):
    return jnp.dot(a, b, preferred_element_type=F32)


def _const_spec(shape, single_buffer=False):
    zeros = (0,) * len(shape)
    if single_buffer:
        return pl.BlockSpec(shape, lambda *_: zeros, pipeline_mode=pl.Buffered(1))
    return pl.BlockSpec(shape, lambda *_: zeros)


def _params(n_grid_dims):
    return pltpu.CompilerParams(
        dimension_semantics=("arbitrary",) * n_grid_dims,
        vmem_limit_bytes=V7X_VMEM_LIMIT_BYTES)


def _convmod_kernel(tq, x_ref, ng_ref, pw1_ref, b1_ref, dww_ref, dwb_ref, lng_ref, lnb_ref,
                    pw2_ref, b2_ref, out_ref, gbuf_ref, ybuf_ref):
    n_ct = D_MODEL // V7X_LANES
    t = pl.program_id(1)
    x = x_ref[0]
    h = _rms(x, ng_ref[...]).astype(BF16)
    u = _dot(h, pw1_ref[...]) + b1_ref[...]
    glu = u[:, :D_MODEL] * _sigmoid(u[:, D_MODEL:])

    @pl.when(t == 0)
    def _():
        gbuf_ref[:, 0:CONV_HALO, :] = jnp.zeros((n_ct, CONV_HALO, V7X_LANES), F32)

    for c in range(n_ct):
        gbuf_ref[c, CONV_HALO:CONV_HALO + tq, :] = glu[:, c * V7X_LANES:(c + 1) * V7X_LANES]

    first_tap_row = CONV_HALO - (CONV_WIDTH - 1)
    n_sub = CONV_ROWS_PER_STEP // V7X_SUBLANES
    for c in range(n_ct):
        lanes = slice(c * V7X_LANES, (c + 1) * V7X_LANES)
        taps = [jnp.broadcast_to(dww_ref[j:j + 1, lanes], (V7X_SUBLANES, V7X_LANES))
                for j in range(CONV_WIDTH)]
        bias = jnp.broadcast_to(dwb_ref[:, lanes], (V7X_SUBLANES, V7X_LANES))

        def conv_rows(i, carry, c=c, lanes=lanes, taps=taps, bias=bias):
            r0 = pl.multiple_of(i * CONV_ROWS_PER_STEP, CONV_ROWS_PER_STEP)
            for k in range(n_sub):
                base = r0 + k * V7X_SUBLANES + first_tap_row
                acc_even = bias
                acc_odd = jnp.zeros((V7X_SUBLANES, V7X_LANES), F32)
                for j in range(CONV_WIDTH):
                    term = taps[j] * gbuf_ref[c, pl.ds(base + j, V7X_SUBLANES), :]
                    if j % 2 == 0:
                        acc_even = acc_even + term
                    else:
                        acc_odd = acc_odd + term
                ybuf_ref[pl.ds(r0 + k * V7X_SUBLANES, V7X_SUBLANES), lanes] = acc_even + acc_odd
            return carry

        lax.fori_loop(0, tq // CONV_ROWS_PER_STEP, conv_rows, 0)

    for c in range(n_ct):
        gbuf_ref[c, 0:CONV_HALO, :] = gbuf_ref[c, tq:tq + CONV_HALO, :]

    y = ybuf_ref[...]
    mu = jnp.mean(y, axis=-1, keepdims=True)
    yc = y - mu
    var = jnp.mean(yc * yc, axis=-1, keepdims=True)
    z = yc * lax.rsqrt(var + EPS) * lng_ref[...] + lnb_ref[...]
    z = (z * _sigmoid(z)).astype(BF16)
    out_ref[0] = x + _dot(z, pw2_ref[...]) + b2_ref[...]


def _conv_module(x, norm_g, pw1_w, pw1_b, dw_w, dw_b, ln_g, ln_b, pw2_w, pw2_b):
    b, s, d = x.shape
    tq = TQ_CONV
    row = lambda v: v.reshape(1, -1)
    x_spec = pl.BlockSpec((1, tq, d), lambda i, t: (i, t, 0))
    return pl.pallas_call(
        functools.partial(_convmod_kernel, tq),
        grid=(b, s // tq),
        in_specs=[x_spec, _const_spec((1, d)), _const_spec((d, 2 * d), True), _const_spec((1, 2 * d)),
                  _const_spec((CONV_WIDTH, d)), _const_spec((1, d)), _const_spec((1, d)),
                  _const_spec((1, d)), _const_spec((d, d), True), _const_spec((1, d))],
        out_specs=x_spec,
        out_shape=jax.ShapeDtypeStruct(x.shape, x.dtype),
        scratch_shapes=[pltpu.VMEM((d // V7X_LANES, CONV_HALO + tq, V7X_LANES), F32),
                        pltpu.VMEM((tq, d), F32)],
        compiler_params=_params(2),
        name="conv_module",
    )(x, row(norm_g), pw1_w.astype(BF16), row(pw1_b), dw_w, row(dw_b), row(ln_g), row(ln_b),
      pw2_w.astype(BF16), row(pw2_b))


def _ffn_kernel(tq, has_attn, *refs):
    if has_attn:
        (x_ref, o_ref, wo_ref, g_ref, win_ref, dww_ref, dwb_ref, wout_ref,
         out_ref, act_ref, carry_ref) = refs
    else:
        (x_ref, g_ref, win_ref, dww_ref, dwb_ref, wout_ref, out_ref, act_ref, carry_ref) = refs
    fc = FFN_CHUNK
    t = pl.program_id(1)

    @pl.when(t == 0)
    def _():
        carry_ref[...] = jnp.zeros(carry_ref.shape, F32)

    x = x_ref[0]
    if has_attn:
        x = x + _dot(o_ref[0], wo_ref[...])
    h = _rms(x, g_ref[...]).astype(BF16)
    row8 = lax.broadcasted_iota(jnp.int32, (V7X_SUBLANES, fc), 0)

    def conv_chunk(lo):
        cols = slice(lo, lo + fc)
        u = _dot(h, win_ref[:, cols])
        prev = carry_ref[:, cols]
        carry_ref[:, cols] = u[tq - V7X_SUBLANES:tq]
        u1 = pltpu.roll(u, 1, 0)
        u2 = pltpu.roll(u, 2, 0)
        head1 = jnp.where(row8 < 1, pltpu.roll(prev, 1, 0), u1[:V7X_SUBLANES])
        head2 = jnp.where(row8 < 2, pltpu.roll(prev, 2, 0), u2[:V7X_SUBLANES])
        u1 = jnp.concatenate([head1, u1[V7X_SUBLANES:]], axis=0)
        u2 = jnp.concatenate([head2, u2[V7X_SUBLANES:]], axis=0)
        w = dww_ref[:, cols]
        return w[0:1] * u2 + w[1:2] * u1 + w[2:3] * u + dwb_ref[:, cols]

    for c in range(D_FF // fc):
        gate = conv_chunk(c * fc)
        up = conv_chunk(D_FF + c * fc)
        act_ref[:, c * fc:(c + 1) * fc] = (gate * _sigmoid(gate) * up).astype(BF16)

    out_ref[0] = x + _dot(act_ref[...], wout_ref[...])


def _conv_ffn(x, norm_g, w_in, dw_w, dw_b, w_out, attn=None):
    b, s, d = x.shape
    tq = TQ_FFN
    x_spec = pl.BlockSpec((1, tq, d), lambda i, t: (i, t, 0))
    in_specs = [x_spec]
    args = [x]
    if attn is not None:
        o, w_o = attn
        in_specs += [pl.BlockSpec((1, tq, o.shape[-1]), lambda i, t: (i, t, 0)),
                     _const_spec(w_o.shape, True)]
        args += [o, w_o.astype(BF16)]
    in_specs += [_const_spec((1, d)), _const_spec((d, 2 * D_FF), True),
                 _const_spec((FFN_CONV_WIDTH, 2 * D_FF)), _const_spec((1, 2 * D_FF)),
                 _const_spec((D_FF, d), True)]
    args += [norm_g.reshape(1, d), w_in.astype(BF16), dw_w, dw_b.reshape(1, -1), w_out.astype(BF16)]
    return pl.pallas_call(
        functools.partial(_ffn_kernel, tq, attn is not None),
        grid=(b, s // tq),
        in_specs=in_specs,
        out_specs=x_spec,
        out_shape=jax.ShapeDtypeStruct(x.shape, x.dtype),
        scratch_shapes=[pltpu.VMEM((tq, D_FF), BF16),
                        pltpu.VMEM((V7X_SUBLANES, 2 * D_FF), F32)],
        compiler_params=_params(2),
        name="conv_ffn_attn" if attn is not None else "conv_ffn",
    )(*args)


def _proj_kernel(n_out, x_ref, g_ref, *refs):
    h = _rms(x_ref[0], g_ref[...]).astype(BF16)
    for w_ref, o_ref in zip(refs[:n_out], refs[n_out:]):
        o_ref[0] = _dot(h, w_ref[...]).astype(o_ref.dtype)


def _project(x, norm_g, weights, out_dtypes):
    b, s, d = x.shape
    tq = TQ_PROJ
    x_spec = pl.BlockSpec((1, tq, d), lambda i, t: (i, t, 0))
    n_out = len(weights)
    return pl.pallas_call(
        functools.partial(_proj_kernel, n_out),
        grid=(b, s // tq),
        in_specs=[x_spec, _const_spec((1, d))] + [_const_spec(w.shape, True) for w in weights],
        out_specs=[pl.BlockSpec((1, tq, w.shape[1]), lambda i, t: (i, t, 0)) for w in weights],
        out_shape=[jax.ShapeDtypeStruct((b, s, w.shape[1]), dt) for w, dt in zip(weights, out_dtypes)],
        compiler_params=_params(2),
        name="project_%d" % n_out,
    )(x, norm_g.reshape(1, d), *[w.astype(BF16) for w in weights])


def _head_rms(x, g2, first_half):
    sq = x * x
    s_lo = jnp.sum(jnp.where(first_half, sq, 0.0), axis=-1, keepdims=True)
    s_hi = jnp.sum(jnp.where(first_half, 0.0, sq), axis=-1, keepdims=True)
    ms = jnp.where(first_half, s_lo, s_hi) * (1.0 / HEAD_DIM)
    return x * lax.rsqrt(ms + EPS) * g2


def _attn_kernel(seq, lam_init, q_ref, k_ref, v_ref, qg_ref, kg_ref, lq1_ref, lk1_ref, lq2_ref,
                 lk2_ref, sg_ref, o_ref, q1_ref, q2_ref, kt_ref):
    bq = Q_BLOCK_ATTN
    lam = (jnp.exp(jnp.sum(lq1_ref[...] * lk1_ref[...], axis=-1, keepdims=True))
           - jnp.exp(jnp.sum(lq2_ref[...] * lk2_ref[...], axis=-1, keepdims=True))
           + lam_init)
    first_half = lax.broadcasted_iota(jnp.int32, (seq, V_DIM), 1) < HEAD_DIM
    qn = _head_rms(q_ref[0], qg_ref[...], first_half) * (HEAD_DIM ** -0.5)
    q1_ref[...] = jnp.where(first_half, qn, 0.0).astype(BF16)
    q2_ref[...] = jnp.where(first_half, 0.0, qn).astype(BF16)
    kn = _head_rms(k_ref[0], kg_ref[...], first_half)
    kt_ref[...] = kn.T.astype(BF16)

    rows = lax.broadcasted_iota(jnp.int32, (bq, bq), 0)
    cols = lax.broadcasted_iota(jnp.int32, (bq, bq), 1)
    causal = cols <= rows

    def softmax(q_blk, kend):
        sc = _dot(q_blk, kt_ref[:, 0:kend])
        diag = jnp.where(causal, sc[:, kend - bq:kend], NEG_INF)
        if kend > bq:
            sc = jnp.concatenate([sc[:, :kend - bq], diag], axis=1)
        else:
            sc = diag
        e = jnp.exp(sc - jnp.max(sc, axis=-1, keepdims=True))
        return e * (1.0 / jnp.sum(e, axis=-1, keepdims=True))

    for i in range(seq // bq):
        kend = (i + 1) * bq
        blk = slice(i * bq, kend)
        p1 = softmax(q1_ref[blk, :], kend)
        p2 = softmax(q2_ref[blk, :], kend)
        diff = (p1 - lam * p2).astype(BF16)
        o = _dot(diff, v_ref[0, 0:kend, :])
        o = _rms(o, sg_ref[...]) * (1.0 - lam_init)
        o_ref[0, blk, :] = o.astype(o_ref.dtype)


def _diff_attention(q, k, v, q_norm_g, k_norm_g, lq1, lk1, lq2, lk2, subln_g, lam_init):
    b, s, _ = q.shape
    head_spec = pl.BlockSpec((1, s, V_DIM), lambda i, h: (i, 0, h))
    row = lambda a: a.reshape(1, -1)
    two = lambda g: jnp.tile(g, 2).reshape(1, 2 * HEAD_DIM)
    small = _const_spec((1, HEAD_DIM))
    return pl.pallas_call(
        functools.partial(_attn_kernel, s, lam_init),
        grid=(b, N_HEADS),
        in_specs=[head_spec, head_spec, head_spec, _const_spec((1, 2 * HEAD_DIM)),
                  _const_spec((1, 2 * HEAD_DIM)), small, small, small, small,
                  _const_spec((1, V_DIM))],
        out_specs=head_spec,
        out_shape=jax.ShapeDtypeStruct((b, s, N_HEADS * V_DIM), BF16),
        scratch_shapes=[pltpu.VMEM((s, 2 * HEAD_DIM), BF16), pltpu.VMEM((s, 2 * HEAD_DIM), BF16),
                        pltpu.VMEM((2 * HEAD_DIM, s), BF16)],
        compiler_params=_params(2),
        name="diff_attention",
    )(q, k, v, two(q_norm_g), two(k_norm_g), row(lq1), row(lk1), row(lq2), row(lk2), row(subln_g))


def kernel(x, conv_norm_g, conv_pw1_w, conv_pw1_b, conv_dw_w, conv_dw_b, conv_ln_g, conv_ln_b,
           conv_pw2_w, conv_pw2_b, kv_norm_g, w_k, w_v, k_norm_g, attn_norm_g, w_q, q_norm_g,
           lambda_q1, lambda_k1, lambda_q2, lambda_k2, subln_g, w_o, ffn_norm_g, ffn_w_in,
           ffn_dw_w, ffn_dw_b, ffn_w_out):
    k_sh = v_sh = None
    for layer in range(DEPTH):
        attn = None
        if layer < N_A_LAYERS:
            i = layer
            x = _conv_module(x, conv_norm_g[i], conv_pw1_w[i], conv_pw1_b[i], conv_dw_w[i],
                             conv_dw_b[i], conv_ln_g[i], conv_ln_b[i], conv_pw2_w[i], conv_pw2_b[i])
        else:
            j = layer - N_A_LAYERS
            if j == 0:
                k_sh, v_sh = _project(x, kv_norm_g, [w_k, w_v], [F32, BF16])
            (q,) = _project(x, attn_norm_g[j], [w_q[j]], [F32])
            o = _diff_attention(q, k_sh, v_sh, q_norm_g[j], k_norm_g, lambda_q1[j], lambda_k1[j],
                                lambda_q2[j], lambda_k2[j], subln_g[j], _lambda_init(layer))
            attn = (o, w_o[j])
        x = _conv_ffn(x, ffn_norm_g[layer], ffn_w_in[layer], ffn_dw_w[layer], ffn_dw_b[layer],
                      ffn_w_out[layer], attn=attn)
    return x
```

```python
import functools
import math

import jax
import jax.numpy as jnp
from jax import lax
from jax.experimental import pallas as pl
from jax.experimental.pallas import tpu as pltpu

D_MODEL = 1024
DEPTH = 4
N_A_LAYERS = DEPTH // 2
CONV_WIDTH = 31
D_FF = 2816
FFN_CONV_WIDTH = 3
N_HEADS = 8
HEAD_DIM = 64
V_DIM = 2 * HEAD_DIM
EPS = 1e-6
NEG_INF = -1e30
LOG2_E = math.log2(math.e)
EXP2_SAFE_SPAN = 100.0

V7X_LANES = 128
V7X_SUBLANES = 8
V7X_MXU_DIM = 256
V7X_VMEM_LIMIT_BYTES = 56 * 1024 * 1024

CONV_HALO = 32
CONV_ROWS_PER_STEP = 32
TQ_CONV = 512
TQ_FFN = 512
TQ_PROJ = 512
FFN_CHUNK = 256
FFN_STAGE_SLOTS = 4
Q_BLOCK_ATTN = 256

BF16 = jnp.bfloat16
F32 = jnp.float32


def _lambda_init(layer_idx):
    return 0.8 - 0.6 * math.exp(-0.3 * layer_idx)


def _rms(x, g):
    ms = jnp.mean(x * x, axis=-1, keepdims=True)
    return x * lax.rsqrt(ms + EPS) * g


def _sigmoid(x):
    return 1.0 / (1.0 + jnp.exp(-x))


def _dot(a, b):
    return jnp.dot(a, b, preferred_element_type=F32)


def _const_spec(shape, single_buffer=False):
    zeros = (0,) * len(shape)
    if single_buffer:
        return pl.BlockSpec(shape, lambda *_: zeros, pipeline_mode=pl.Buffered(1))
    return pl.BlockSpec(shape, lambda *_: zeros)


def _params(n_grid_dims):
    return pltpu.CompilerParams(
        dimension_semantics=("arbitrary",) * n_grid_dims,
        vmem_limit_bytes=V7X_VMEM_LIMIT_BYTES)


def _convmod_kernel(tq, x_ref, ng_ref, pw1_ref, b1_ref, dww_ref, dwb_ref, lng_ref, lnb_ref,
                    pw2_ref, b2_ref, out_ref, gbuf_ref, ybuf_ref):
    n_ct = D_MODEL // V7X_LANES
    t = pl.program_id(1)
    x = x_ref[0]
    h = _rms(x, ng_ref[...]).astype(BF16)
    u = _dot(h, pw1_ref[...]) + b1_ref[...]
    glu = u[:, :D_MODEL] * _sigmoid(u[:, D_MODEL:])

    @pl.when(t == 0)
    def _():
        gbuf_ref[:, 0:CONV_HALO, :] = jnp.zeros((n_ct, CONV_HALO, V7X_LANES), F32)

    for c in range(n_ct):
        gbuf_ref[c, CONV_HALO:CONV_HALO + tq, :] = glu[:, c * V7X_LANES:(c + 1) * V7X_LANES]

    first_tap_row = CONV_HALO - (CONV_WIDTH - 1)
    n_sub = CONV_ROWS_PER_STEP // V7X_SUBLANES
    for c in range(n_ct):
        lanes = slice(c * V7X_LANES, (c + 1) * V7X_LANES)
        taps = [jnp.broadcast_to(dww_ref[j:j + 1, lanes], (V7X_SUBLANES, V7X_LANES))
                for j in range(CONV_WIDTH)]
        bias = jnp.broadcast_to(dwb_ref[:, lanes], (V7X_SUBLANES, V7X_LANES))

        def conv_rows(i, carry, c=c, lanes=lanes, taps=taps, bias=bias):
            r0 = pl.multiple_of(i * CONV_ROWS_PER_STEP, CONV_ROWS_PER_STEP)
            for k in range(n_sub):
                base = r0 + k * V7X_SUBLANES + first_tap_row
                acc_even = bias
                acc_odd = jnp.zeros((V7X_SUBLANES, V7X_LANES), F32)
                for j in range(CONV_WIDTH):
                    term = taps[j] * gbuf_ref[c, pl.ds(base + j, V7X_SUBLANES), :]
                    if j % 2 == 0:
                        acc_even = acc_even + term
                    else:
                        acc_odd = acc_odd + term
                ybuf_ref[pl.ds(r0 + k * V7X_SUBLANES, V7X_SUBLANES), lanes] = acc_even + acc_odd
            return carry

        lax.fori_loop(0, tq // CONV_ROWS_PER_STEP, conv_rows, 0)

    for c in range(n_ct):
        gbuf_ref[c, 0:CONV_HALO, :] = gbuf_ref[c, tq:tq + CONV_HALO, :]

    y = ybuf_ref[...]
    mu = jnp.mean(y, axis=-1, keepdims=True)
    yc = y - mu
    var = jnp.mean(yc * yc, axis=-1, keepdims=True)
    z = yc * lax.rsqrt(var + EPS) * lng_ref[...] + lnb_ref[...]
    z = (z * _sigmoid(z)).astype(BF16)
    out_ref[0] = x + _dot(z, pw2_ref[...]) + b2_ref[...]


def _conv_module(x, norm_g, pw1_w, pw1_b, dw_w, dw_b, ln_g, ln_b, pw2_w, pw2_b):
    b, s, d = x.shape
    tq = TQ_CONV
    row = lambda v: v.reshape(1, -1)
    x_spec = pl.BlockSpec((1, tq, d), lambda i, t: (i, t, 0))
    return pl.pallas_call(
        functools.partial(_convmod_kernel, tq),
        grid=(b, s // tq),
        in_specs=[x_spec, _const_spec((1, d)), _const_spec((d, 2 * d), True), _const_spec((1, 2 * d)),
                  _const_spec((CONV_WIDTH, d)), _const_spec((1, d)), _const_spec((1, d)),
                  _const_spec((1, d)), _const_spec((d, d), True), _const_spec((1, d))],
        out_specs=x_spec,
        out_shape=jax.ShapeDtypeStruct(x.shape, x.dtype),
        scratch_shapes=[pltpu.VMEM((d // V7X_LANES, CONV_HALO + tq, V7X_LANES), F32),
                        pltpu.VMEM((tq, d), F32)],
        compiler_params=_params(2),
        name="conv_module",
    )(x, row(norm_g), pw1_w.astype(BF16), row(pw1_b), dw_w, row(dw_b), row(ln_g), row(ln_b),
      pw2_w.astype(BF16), row(pw2_b))


def _ffn_kernel(tq, has_attn, *refs):
    if has_attn:
        (x_ref, o_ref, wo_ref, g_ref, win_ref, dww_ref, dwb_ref, wout_ref,
         out_ref, act_ref, carry_ref, ubuf_ref) = refs
    else:
        (x_ref, g_ref, win_ref, dww_ref, dwb_ref, wout_ref, out_ref, act_ref, carry_ref,
         ubuf_ref) = refs
    fc = FFN_CHUNK
    halo = V7X_SUBLANES
    t = pl.program_id(1)

    @pl.when(t == 0)
    def _():
        carry_ref[...] = jnp.zeros(carry_ref.shape, F32)

    x = x_ref[0]
    if has_attn:
        x = x + _dot(o_ref[0], wo_ref[...])
    h = _rms(x, g_ref[...]).astype(BF16)

    def conv_chunk(lo, slot):
        u = _dot(h, win_ref[:, lo:lo + fc])
        outs = []
        for k in range(fc // V7X_LANES):
            cols = slice(lo + k * V7X_LANES, lo + (k + 1) * V7X_LANES)
            tile = lo // V7X_LANES + k
            uk = u[:, k * V7X_LANES:(k + 1) * V7X_LANES]
            ubuf_ref[slot, k, 0:halo, :] = carry_ref[tile]
            ubuf_ref[slot, k, halo:halo + tq, :] = uk
            carry_ref[tile] = uk[tq - halo:tq]
            w = dww_ref[:, cols]
            outs.append(w[0:1] * ubuf_ref[slot, k, halo - 2:halo - 2 + tq, :]
                        + w[1:2] * ubuf_ref[slot, k, halo - 1:halo - 1 + tq, :]
                        + w[2:3] * uk + dwb_ref[:, cols])
        return jnp.concatenate(outs, axis=1)

    for c in range(D_FF // fc):
        slot = 2 * (c % (FFN_STAGE_SLOTS // 2))
        gate = conv_chunk(c * fc, slot)
        up = conv_chunk(D_FF + c * fc, slot + 1)
        act_ref[:, c * fc:(c + 1) * fc] = (gate * _sigmoid(gate) * up).astype(BF16)

    out_ref[0] = x + _dot(act_ref[...], wout_ref[...])


def _conv_ffn(x, norm_g, w_in, dw_w, dw_b, w_out, attn=None):
    b, s, d = x.shape
    tq = TQ_FFN
    x_spec = pl.BlockSpec((1, tq, d), lambda i, t: (i, t, 0))
    in_specs = [x_spec]
    args = [x]
    if attn is not None:
        o, w_o = attn
        in_specs += [pl.BlockSpec((1, tq, o.shape[-1]), lambda i, t: (i, t, 0)),
                     _const_spec(w_o.shape, True)]
        args += [o, w_o.astype(BF16)]
    in_specs += [_const_spec((1, d)), _const_spec((d, 2 * D_FF), True),
                 _const_spec((FFN_CONV_WIDTH, 2 * D_FF)), _const_spec((1, 2 * D_FF)),
                 _const_spec((D_FF, d), True)]
    args += [norm_g.reshape(1, d), w_in.astype(BF16), dw_w, dw_b.reshape(1, -1), w_out.astype(BF16)]
    return pl.pallas_call(
        functools.partial(_ffn_kernel, tq, attn is not None),
        grid=(b, s // tq),
        in_specs=in_specs,
        out_specs=x_spec,
        out_shape=jax.ShapeDtypeStruct(x.shape, x.dtype),
        scratch_shapes=[pltpu.VMEM((tq, D_FF), BF16),
                        pltpu.VMEM((2 * D_FF // V7X_LANES, V7X_SUBLANES, V7X_LANES), F32),
                        pltpu.VMEM((FFN_STAGE_SLOTS, FFN_CHUNK // V7X_LANES, V7X_SUBLANES + tq,
                                    V7X_LANES), F32)],
        compiler_params=_params(2),
        name="conv_ffn_attn" if attn is not None else "conv_ffn",
    )(*args)


def _head_rms(u, gain_ref, group_mean_ref):
    sq = u * u
    hi = sq.astype(BF16)
    lo = (sq - hi.astype(F32)).astype(BF16)
    gm = group_mean_ref[...]
    ms = jnp.concatenate(
        [_dot(hi[:, c:c + V7X_MXU_DIM], gm) + _dot(lo[:, c:c + V7X_MXU_DIM], gm)
         for c in range(0, u.shape[1], V7X_MXU_DIM)], axis=1)
    return u * lax.rsqrt(ms + EPS) * gain_ref[...]


def _q_proj_kernel(x_ref, g_ref, w_ref, hg_ref, gm_ref, q_ref):
    h = _rms(x_ref[0], g_ref[...]).astype(BF16)
    qn = _head_rms(_dot(h, w_ref[...]), hg_ref, gm_ref)
    q_ref[0] = (qn * (HEAD_DIM ** -0.5 * LOG2_E)).astype(q_ref.dtype)


def _kv_proj_kernel(x_ref, g_ref, wk_ref, wv_ref, hg_ref, gm_ref, kt_ref, v_ref):
    h = _rms(x_ref[0], g_ref[...]).astype(BF16)
    kn = _head_rms(_dot(h, wk_ref[...]), hg_ref, gm_ref)
    kt_ref[0] = kn.T.astype(kt_ref.dtype)
    v_ref[0] = _dot(h, wv_ref[...]).astype(v_ref.dtype)


def _group_mean_matrix():
    group = jnp.arange(V7X_MXU_DIM) // HEAD_DIM
    return jnp.where(group[:, None] == group[None, :], 1.0 / HEAD_DIM, 0.0).astype(BF16)


def _project_q(x, norm_g, w_q, head_g):
    b, s, d = x.shape
    tq = TQ_PROJ
    x_spec = pl.BlockSpec((1, tq, d), lambda i, t: (i, t, 0))
    return pl.pallas_call(
        _q_proj_kernel,
        grid=(b, s // tq),
        in_specs=[x_spec, _const_spec((1, d)), _const_spec(w_q.shape, True), _const_spec((1, d)),
                  _const_spec((V7X_MXU_DIM, V7X_MXU_DIM))],
        out_specs=x_spec,
        out_shape=jax.ShapeDtypeStruct((b, s, d), BF16),
        compiler_params=_params(2),
        name="project_q",
    )(x, norm_g.reshape(1, d), w_q.astype(BF16), jnp.tile(head_g, d // HEAD_DIM).reshape(1, d),
      _group_mean_matrix())


def _project_kv(x, norm_g, w_k, w_v, head_g):
    b, s, d = x.shape
    tq = TQ_PROJ
    x_spec = pl.BlockSpec((1, tq, d), lambda i, t: (i, t, 0))
    return pl.pallas_call(
        _kv_proj_kernel,
        grid=(b, s // tq),
        in_specs=[x_spec, _const_spec((1, d)), _const_spec(w_k.shape, True),
                  _const_spec(w_v.shape, True), _const_spec((1, d)),
                  _const_spec((V7X_MXU_DIM, V7X_MXU_DIM))],
        out_specs=[pl.BlockSpec((1, d, tq), lambda i, t: (i, 0, t)), x_spec],
        out_shape=[jax.ShapeDtypeStruct((b, d, s), BF16), jax.ShapeDtypeStruct((b, s, d), BF16)],
        compiler_params=_params(2),
        name="project_kv",
    )(x, norm_g.reshape(1, d), w_k.astype(BF16), w_v.astype(BF16),
      jnp.tile(head_g, d // HEAD_DIM).reshape(1, d), _group_mean_matrix())


def _attn_kernel(seq, lam_init, q_ref, k_ref, v_ref, qg_ref, kg_ref, lq1_ref, lk1_ref, lq2_ref,
                 lk2_ref, sg_ref, o_ref, q1_ref, q2_ref, kt_ref):
    bq = Q_BLOCK_ATTN
    lam = (jnp.exp(jnp.sum(lq1_ref[...] * lk1_ref[...], axis=-1, keepdims=True))
           - jnp.exp(jnp.sum(lq2_ref[...] * lk2_ref[...], axis=-1, keepdims=True))
           + lam_init)
    width = 2 * HEAD_DIM
    lane = lax.broadcasted_iota(jnp.int32, (seq, width), 1)
    first_half = lane < HEAD_DIM
    qn = q_ref[0].astype(F32)

    q_norm_max = math.sqrt(HEAD_DIM) * (HEAD_DIM ** -0.5 * LOG2_E) * jnp.max(jnp.abs(qg_ref[...]))
    k_norm_max = math.sqrt(HEAD_DIM) * jnp.max(jnp.abs(kg_ref[...]))
    bound = q_norm_max * k_norm_max
    ones_col = jnp.where(lane == 0, 1.0, 0.0)
    q1_ref[...] = jnp.concatenate([jnp.where(first_half, qn, 0.0), ones_col], axis=1).astype(BF16)
    q2_ref[...] = jnp.concatenate([jnp.where(first_half, 0.0, qn), ones_col], axis=1).astype(BF16)
    kt_ref[0:width, :] = k_ref[0]
    shift_row = lax.broadcasted_iota(jnp.int32, (width, seq), 0) == 0
    kt_ref[width:2 * width, :] = jnp.where(shift_row, -bound, 0.0).astype(BF16)
    bound_is_tight = 2.0 * bound <= EXP2_SAFE_SPAN

    rows = lax.broadcasted_iota(jnp.int32, (bq, bq), 0)
    cols = lax.broadcasted_iota(jnp.int32, (bq, bq), 1)
    causal = cols <= rows

    def exp_scores(q_scr, blk, kend, use_bound):
        kdim = 2 * width if use_bound else width
        sc = _dot(q_scr[blk, 0:kdim], kt_ref[0:kdim, 0:kend])
        diag = jnp.where(causal, sc[:, kend - bq:kend], NEG_INF)
        if kend > bq:
            sc = jnp.concatenate([sc[:, :kend - bq], diag], axis=1)
        else:
            sc = diag
        if not use_bound:
            sc = sc - jnp.max(sc, axis=-1, keepdims=True)
        e = jnp.exp2(sc)
        return e, jnp.sum(e, axis=-1, keepdims=True)

    def attend(use_bound):
        for i in range(seq // bq):
            kend = (i + 1) * bq
            blk = slice(i * bq, kend)
            e1, l1 = exp_scores(q1_ref, blk, kend, use_bound)
            e2, l2 = exp_scores(q2_ref, blk, kend, use_bound)
            diff = (e1 * (1.0 / l1) - e2 * (lam / l2)).astype(BF16)
            o = _dot(diff, v_ref[0, 0:kend, :])
            o = _rms(o, sg_ref[...]) * (1.0 - lam_init)
            o_ref[0, blk, :] = o.astype(o_ref.dtype)

    pl.when(bound_is_tight)(functools.partial(attend, True))
    pl.when(jnp.logical_not(bound_is_tight))(functools.partial(attend, False))


def _diff_attention(q, k_t, v, q_norm_g, k_norm_g, lq1, lk1, lq2, lk2, subln_g, lam_init):
    b, s, _ = q.shape
    head_spec = pl.BlockSpec((1, s, V_DIM), lambda i, h: (i, 0, h))
    head_t_spec = pl.BlockSpec((1, 2 * HEAD_DIM, s), lambda i, h: (i, h, 0))
    row = lambda a: a.reshape(1, -1)
    two = lambda g: jnp.tile(g, 2).reshape(1, 2 * HEAD_DIM)
    small = _const_spec((1, HEAD_DIM))
    return pl.pallas_call(
        functools.partial(_attn_kernel, s, lam_init),
        grid=(b, N_HEADS),
        in_specs=[head_spec, head_t_spec, head_spec, _const_spec((1, 2 * HEAD_DIM)),
                  _const_spec((1, 2 * HEAD_DIM)), small, small, small, small,
                  _const_spec((1, V_DIM))],
        out_specs=head_spec,
        out_shape=jax.ShapeDtypeStruct((b, s, N_HEADS * V_DIM), BF16),
        scratch_shapes=[pltpu.VMEM((s, 4 * HEAD_DIM), BF16), pltpu.VMEM((s, 4 * HEAD_DIM), BF16),
                        pltpu.VMEM((4 * HEAD_DIM, s), BF16)],
        compiler_params=_params(2),
        name="diff_attention",
    )(q, k_t, v, two(q_norm_g), two(k_norm_g), row(lq1), row(lk1), row(lq2), row(lk2),
      row(subln_g))


def kernel(x, conv_norm_g, conv_pw1_w, conv_pw1_b, conv_dw_w, conv_dw_b, conv_ln_g, conv_ln_b,
           conv_pw2_w, conv_pw2_b, kv_norm_g, w_k, w_v, k_norm_g, attn_norm_g, w_q, q_norm_g,
           lambda_q1, lambda_k1, lambda_q2, lambda_k2, subln_g, w_o, ffn_norm_g, ffn_w_in,
           ffn_dw_w, ffn_dw_b, ffn_w_out):
    k_sh = v_sh = None
    for layer in range(DEPTH):
        attn = None
        if layer < N_A_LAYERS:
            i = layer
            x = _conv_module(x, conv_norm_g[i], conv_pw1_w[i], conv_pw1_b[i], conv_dw_w[i],
                             conv_dw_b[i], conv_ln_g[i], conv_ln_b[i], conv_pw2_w[i], conv_pw2_b[i])
        else:
            j = layer - N_A_LAYERS
            if j == 0:
                k_sh, v_sh = _project_kv(x, kv_norm_g, w_k, w_v, k_norm_g)
            q = _project_q(x, attn_norm_g[j], w_q[j], q_norm_g[j])
            o = _diff_attention(q, k_sh, v_sh, q_norm_g[j], k_norm_g, lambda_q1[j], lambda_k1[j],
                                lambda_q2[j], lambda_k2[j], subln_g[j], _lambda_init(layer))
            attn = (o, w_o[j])
        x = _conv_ffn(x, ffn_norm_g[layer], ffn_w_in[layer], ffn_dw_w[layer], ffn_dw_b[layer],
                      ffn_w_out[layer], attn=attn)
    return x
```
